```python
import math
import jax, jax.numpy as jnp
from jax import lax
import numpy as np

D_MODEL = 1024
BATCH = 4
SEQ = 4096
DEPTH = 1

D_MIX = D_MODEL
LRU_WIDTH = D_MIX // 2
LRU_BLOCKS = 8
LRU_BLOCK = LRU_WIDTH // LRU_BLOCKS
CONV_WIDTH = 4
LRU_C = 8.0
ATT_HEAD_DIM = 64
ATT_WIDTH = D_MIX - LRU_WIDTH
ATT_HEADS = ATT_WIDTH // (2 * ATT_HEAD_DIM)
ATT_V_DIM = 2 * ATT_HEAD_DIM
Q_BLOCK = 128
ROPE_THETA = 10000.0
D_FF = 4 * D_MODEL
NORM_EPS = 1e-6
IN_COLS = 2 * LRU_WIDTH + 3 * ATT_WIDTH

kernel_name = "hymba_style_rglru_diffattn_layer"


def rms_norm(x, g):
    xf = x.astype(jnp.float32)
    y = xf * lax.rsqrt(jnp.mean(xf * xf, axis=-1, keepdims=True) + NORM_EPS)
    return (y * g.astype(jnp.float32)).astype(x.dtype)


def lambda_init(layer_idx):
    return 0.8 - 0.6 * math.exp(-0.3 * layer_idx)


def apply_rope(t, pos):
    half = t.shape[-1] // 2
    freqs = ROPE_THETA ** (-jnp.arange(half, dtype=jnp.float32) / half)
    ang = pos.astype(jnp.float32)[:, None] * freqs[None, :]
    cos = jnp.cos(ang)[None, :, None, None, :]
    sin = jnp.sin(ang)[None, :, None, None, :]
    tf = t.astype(jnp.float32)
    t1, t2 = tf[..., :half], tf[..., half:]
    out = jnp.concatenate([t1 * cos - t2 * sin, t2 * cos + t1 * sin], axis=-1)
    return out.astype(t.dtype)


def rglru_group(u, gate_in, conv_w, conv_b, w_gate_a, b_gate_a, w_gate_x, b_gate_x, lru_lambda):
    B, T, C = u.shape
    xc = lax.conv_general_dilated(
        u, conv_w[:, None, :].astype(u.dtype), window_strides=(1,),
        padding=[(CONV_WIDTH - 1, 0)], dimension_numbers=("NWC", "WIO", "NWC"),
        feature_group_count=C) + conv_b
    xb = xc.reshape(B, T, LRU_BLOCKS, LRU_BLOCK)
    r = jax.nn.sigmoid(jnp.einsum("bthi,hij->bthj", xb, w_gate_a).reshape(B, T, C) + b_gate_a)
    i = jax.nn.sigmoid(jnp.einsum("bthi,hij->bthj", xb, w_gate_x).reshape(B, T, C) + b_gate_x)
    log_a = -LRU_C * r.astype(jnp.float32) * jax.nn.softplus(-lru_lambda.astype(jnp.float32))
    a = jnp.exp(log_a)
    b = jnp.sqrt(-jnp.expm1(2.0 * log_a)) * (i * xc).astype(jnp.float32)

    def combine(e1, e2):
        a1, b1 = e1
        a2, b2 = e2
        return a1 * a2, a2 * b1 + b2

    _, hseq = lax.associative_scan(combine, (a, b), axis=1)
    return hseq.astype(u.dtype) * jax.nn.gelu(gate_in)


def diff_attn_group(q, k, v, lq1, lk1, lq2, lk2, head_gain, lam_init):
    B, T, _ = q.shape
    pos = jnp.arange(T)
    q = apply_rope(q.reshape(B, T, ATT_HEADS, 2, ATT_HEAD_DIM), pos)
    k = apply_rope(k.reshape(B, T, ATT_HEADS, 2, ATT_HEAD_DIM), pos)
    v = v.reshape(B, T, ATT_HEADS, ATT_V_DIM)
    f32 = jnp.float32
    lam = (jnp.exp(jnp.sum(lq1.astype(f32) * lk1.astype(f32)))
           - jnp.exp(jnp.sum(lq2.astype(f32) * lk2.astype(f32))) + lam_init)
    scale = ATT_HEAD_DIM ** -0.5
    nb = T // Q_BLOCK
    qb = q.reshape(B, nb, Q_BLOCK, ATT_HEADS, 2, ATT_HEAD_DIM).transpose(1, 0, 2, 3, 4, 5)
    kpos = jnp.arange(T)

    def one_block(args):
        qblk, bi = args
        s = jnp.einsum("bqhmd,bkhmd->bhmqk", qblk, k).astype(f32) * scale
        qpos = bi * Q_BLOCK + jnp.arange(Q_BLOCK)
        causal = kpos[None, :] <= qpos[:, None]
        s = jnp.where(causal, s, jnp.float32(-1e30))
        p = jax.nn.softmax(s, axis=-1)
        w = p[:, :, 0] - lam * p[:, :, 1]
        return jnp.einsum("bhqk,bkhe->bqhe", w.astype(v.dtype), v)

    o = lax.map(one_block, (qb, jnp.arange(nb)))
    o = o.transpose(1, 0, 2, 3, 4).reshape(B, T, ATT_HEADS, ATT_V_DIM)
    o = rms_norm(o, head_gain) * (1.0 - lam_init)
    return o.reshape(B, T, ATT_WIDTH)


def setup_inputs(seed: int = 0) -> dict:
    key = jax.random.key(seed)
    ks = jax.random.split(key, 24)
    f32 = jnp.float32
    L = DEPTH

    def nrm(k, shape, scale):
        return jax.random.normal(k, shape, f32) * scale

    def gain(k, n):
        return 1.0 + 0.02 * jax.random.normal(k, (L, n), f32)

    u = jax.random.uniform(ks[10], (L, LRU_WIDTH), f32, 0.9, 0.999)
    a0 = u ** (1.0 / LRU_C)
    lru_lambda = jnp.log(a0) - jnp.log1p(-a0)
    return {
        "x": jax.random.normal(ks[0], (BATCH, SEQ, D_MODEL), f32),
        "norm_mix_pre": gain(ks[1], D_MODEL),
        "w_in": nrm(ks[2], (L, D_MODEL, IN_COLS), D_MODEL ** -0.5),
        "conv_w": nrm(ks[3], (L, CONV_WIDTH, LRU_WIDTH), CONV_WIDTH ** -0.5),
        "conv_b": nrm(ks[4], (L, LRU_WIDTH), 0.01),
        "w_gate_a": nrm(ks[5], (L, LRU_BLOCKS, LRU_BLOCK, LRU_BLOCK), LRU_BLOCK ** -0.5),
        "b_gate_a": nrm(ks[6], (L, LRU_WIDTH), 0.01),
        "w_gate_x": nrm(ks[7], (L, LRU_BLOCKS, LRU_BLOCK, LRU_BLOCK), LRU_BLOCK ** -0.5),
        "b_gate_x": nrm(ks[8], (L, LRU_WIDTH), 0.01),
        "lru_lambda": lru_lambda,
        "lambda_q1": nrm(ks[11], (L, ATT_HEAD_DIM), 0.1),
        "lambda_k1": nrm(ks[12], (L, ATT_HEAD_DIM), 0.1),
        "lambda_q2": nrm(ks[13], (L, ATT_HEAD_DIM), 0.1),
        "lambda_k2": nrm(ks[14], (L, ATT_HEAD_DIM), 0.1),
        "att_head_norm": gain(ks[15], ATT_V_DIM),
        "w_out": nrm(ks[16], (L, D_MIX, D_MODEL), D_MIX ** -0.5),
        "norm_mix_post": gain(ks[17], D_MODEL),
        "norm_mlp_pre": gain(ks[18], D_MODEL),
        "w_up": nrm(ks[19], (L, D_MODEL, D_FF), D_MODEL ** -0.5),
        "w_down": nrm(ks[20], (L, D_FF, D_MODEL), D_FF ** -0.5),
        "norm_mlp_post": gain(ks[21], D_MODEL),
    }


def reference(x, norm_mix_pre, w_in, conv_w, conv_b, w_gate_a, b_gate_a, w_gate_x, b_gate_x,
              lru_lambda, lambda_q1, lambda_k1, lambda_q2, lambda_k2, att_head_norm, w_out,
              norm_mix_post, norm_mlp_pre, w_up, w_down, norm_mlp_post):
    splits = [LRU_WIDTH, 2 * LRU_WIDTH, 2 * LRU_WIDTH + ATT_WIDTH, 2 * LRU_WIDTH + 2 * ATT_WIDTH]
    for l in range(DEPTH):
        hn = rms_norm(x, norm_mix_pre[l])
        proj = jnp.einsum("btd,de->bte", hn, w_in[l])
        lru_x, lru_gate, q, k, v = jnp.split(proj, splits, axis=-1)
        y_lru = rglru_group(lru_x, lru_gate, conv_w[l], conv_b[l], w_gate_a[l], b_gate_a[l],
                            w_gate_x[l], b_gate_x[l], lru_lambda[l])
        y_att = diff_attn_group(q, k, v, lambda_q1[l], lambda_k1[l], lambda_q2[l], lambda_k2[l],
                                att_head_norm[l], lambda_init(l))
        mixed = jnp.concatenate([y_lru, y_att], axis=-1)
        x = x + rms_norm(jnp.einsum("bte,ed->btd", mixed, w_out[l]), norm_mix_post[l])
        hm = rms_norm(x, norm_mlp_pre[l])
        act = jnp.square(jax.nn.relu(jnp.einsum("btd,df->btf", hm, w_up[l])))
        x = x + rms_norm(jnp.einsum("btf,fd->btd", act, w_down[l]), norm_mlp_post[l])
    return x
```

```python
import functools
import math

import jax
import jax.numpy as jnp
from jax import lax
from jax.experimental import pallas as pl
from jax.experimental.pallas import tpu as pltpu

F32 = jnp.float32
BF16 = jnp.bfloat16

NORM_EPS = 1e-6
LRU_C = 8.0
CONV_WIDTH = 4
LRU_BLOCKS = 8
HEAD_DIM = 64
V_DIM = 2 * HEAD_DIM
ROPE_THETA = 10000.0
MASK_VALUE = -1e30

SUBLANES = 8
ROW_TILE = 512
ATT_TQ = 512
ATT_TK = 512
FF_CHUNK = 1024
VMEM_LIMIT = 56 * 1024 * 1024


def _rms(x, g):
    ms = jnp.mean(x * x, axis=-1, keepdims=True)
    return x * lax.rsqrt(ms + NORM_EPS) * g


def _dot(a, b):
    return jnp.dot(a, b, preferred_element_type=F32)


def _mix_in_kernel(x_ref, g_ref, w_ref, cw_ref, cb_ref, wa_ref, ba_ref, wx_ref, bx_ref, lam_ref,
                   cos_ref, sin_lo_ref, sin_hi_ref,
                   ylru_ref, q_ref, k_ref, v_ref,
                   uext_ref, a_ref, b_ref, hcar_ref, *, lru_w, att_w):
    t = pl.program_id(1)
    tm = x_ref.shape[0]

    @pl.when(t == 0)
    def _():
        uext_ref[0:SUBLANES, :] = jnp.zeros((SUBLANES, lru_w), F32)
        hcar_ref[...] = jnp.zeros_like(hcar_ref)

    hn = _rms(x_ref[...], g_ref[...]).astype(BF16)

    c_q, c_k, c_v = 2 * lru_w, 2 * lru_w + att_w, 2 * lru_w + 2 * att_w
    cos, sin_lo, sin_hi = cos_ref[...], sin_lo_ref[...], sin_hi_ref[...]

    def rope(th):
        up = pltpu.roll(th, V_DIM - HEAD_DIM // 2, 1)
        dn = pltpu.roll(th, HEAD_DIM // 2, 1)
        return th * cos + up * sin_lo + dn * sin_hi

    qf = _dot(hn, w_ref[:, c_q:c_k]) * (HEAD_DIM ** -0.5)
    kf = _dot(hn, w_ref[:, c_k:c_v])
    for h in range(att_w // V_DIM):
        sl = slice(h * V_DIM, (h + 1) * V_DIM)
        q_ref[:, sl] = rope(qf[:, sl]).astype(BF16)
        k_ref[:, sl] = rope(kf[:, sl]).astype(BF16)
    v_ref[...] = _dot(hn, w_ref[:, c_v:c_v + att_w]).astype(BF16)

    u = _dot(hn, w_ref[:, 0:lru_w])
    uext_ref[SUBLANES:SUBLANES + tm, :] = u
    xc = cb_ref[...] + cw_ref[CONV_WIDTH - 1:CONV_WIDTH, :] * u
    for j in range(1, CONV_WIDTH):
        xc = xc + cw_ref[CONV_WIDTH - 1 - j:CONV_WIDTH - j, :] * uext_ref[SUBLANES - j:SUBLANES - j + tm, :]
    uext_ref[0:SUBLANES, :] = uext_ref[tm:tm + SUBLANES, :]

    xcb = xc.astype(BF16)
    r = jax.nn.sigmoid(_dot(xcb, wa_ref[...]) + ba_ref[...])
    i = jax.nn.sigmoid(_dot(xcb, wx_ref[...]) + bx_ref[...])
    neg_lam = -lam_ref[...]
    softplus = jnp.maximum(neg_lam, 0.0) + jnp.log1p(jnp.exp(-jnp.abs(neg_lam)))
    log_a = (-LRU_C) * r * softplus
    a_ref[...] = jnp.exp(log_a)
    th = jnp.tanh(log_a)
    b_ref[...] = jnp.sqrt(-2.0 * th / (1.0 - th)) * (i * xc)

    row = lax.broadcasted_iota(jnp.int32, (SUBLANES, lru_w), 0)
    steps = [(s, row >= s) for s in (1, 2, 4)]

    def group(gidx, hprev):
        r0 = pl.multiple_of(gidx * SUBLANES, SUBLANES)
        a8 = a_ref[pl.ds(r0, SUBLANES), :]
        b8 = b_ref[pl.ds(r0, SUBLANES), :]
        for s, m in steps:
            a_sh = jnp.where(m, pltpu.roll(a8, s, 0), 1.0)
            b_sh = jnp.where(m, pltpu.roll(b8, s, 0), 0.0)
            b8 = a8 * b_sh + b8
            a8 = a8 * a_sh
        h8 = a8 * hprev + b8
        b_ref[pl.ds(r0, SUBLANES), :] = h8
        return jnp.broadcast_to(h8[SUBLANES - 1:SUBLANES, :], (SUBLANES, lru_w))

    hcar_ref[...] = lax.fori_loop(0, tm // SUBLANES, group, hcar_ref[...], unroll=4)

    gate = _dot(hn, w_ref[:, lru_w:2 * lru_w])
    ylru_ref[...] = (b_ref[...] * jax.nn.gelu(gate)).astype(BF16)


def _mix_in(x2, g, w_in, conv_w, conv_b, wa, ba, wx, bx, lam, cos, sin_lo, sin_hi, *, batch, seq):
    n, d = x2.shape
    lru_w = conv_w.shape[1]
    att_w = (w_in.shape[1] - 2 * lru_w) // 3
    tm = ROW_TILE
    nt = seq // tm
    row_blk = lambda w: pl.BlockSpec((tm, w), lambda b, t: (b * nt + t, 0))
    full = lambda a: pl.BlockSpec(a.shape, lambda b, t: (0,) * a.ndim)
    tab = pl.BlockSpec((tm, V_DIM), lambda b, t: (t, 0))
    out = jax.ShapeDtypeStruct((n, lru_w), BF16)
    return pl.pallas_call(
        functools.partial(_mix_in_kernel, lru_w=lru_w, att_w=att_w),
        grid=(batch, nt),
        in_specs=[row_blk(d), full(g), full(w_in), full(conv_w), full(conv_b), full(wa), full(ba),
                  full(wx), full(bx), full(lam), tab, tab, tab],
        out_specs=[row_blk(lru_w), row_blk(att_w), row_blk(att_w), row_blk(att_w)],
        out_shape=[out, jax.ShapeDtypeStruct((n, att_w), BF16), jax.ShapeDtypeStruct((n, att_w), BF16),
                   jax.ShapeDtypeStruct((n, att_w), BF16)],
        scratch_shapes=[pltpu.VMEM((tm + SUBLANES, lru_w), F32), pltpu.VMEM((tm, lru_w), F32),
                        pltpu.VMEM((tm, lru_w), F32), pltpu.VMEM((SUBLANES, lru_w), F32)],
        compiler_params=pltpu.CompilerParams(dimension_semantics=("arbitrary", "arbitrary"),
                                             vmem_limit_bytes=VMEM_LIMIT),
        name="mix_in",
    )(x2, g, w_in, conv_w, conv_b, wa, ba, wx, bx, lam, cos, sin_lo, sin_hi)


def _attn_kernel(lq1_ref, lk1_ref, lq2_ref, lk2_ref, gain_ref, qt_ref, k_ref, vt_ref, o_ref,
                 acc1_ref, acc2_ref, *, lam_init):
    qi = pl.program_id(2)
    tq = qt_ref.shape[1]
    tk = tq

    lam = (jnp.exp(jnp.sum(lq1_ref[...] * lk1_ref[...], axis=-1, keepdims=True))
           - jnp.exp(jnp.sum(lq2_ref[...] * lk2_ref[...], axis=-1, keepdims=True)) + lam_init)

    qt = qt_ref[...]
    drow = lax.broadcasted_iota(jnp.int32, qt.shape, 0)
    zero = jnp.zeros_like(qt)
    qt1 = jnp.where(drow < HEAD_DIM, qt, zero)
    qt2 = jnp.where(drow >= HEAD_DIM, qt, zero)

    acc1_ref[...] = jnp.zeros_like(acc1_ref)
    acc2_ref[...] = jnp.zeros_like(acc2_ref)

    def block(j, carry, masked):
        m1, l1, m2, l2 = carry
        k0 = pl.multiple_of(j * tk, tk)
        kb = k_ref[pl.ds(k0, tk), :]
        vtb = vt_ref[:, pl.ds(k0, tk)]
        if masked:
            kpos = lax.broadcasted_iota(jnp.int32, (tk, tq), 0)
            qpos = lax.broadcasted_iota(jnp.int32, (tk, tq), 1)
            keep = kpos <= qpos
        out = []
        for qm, m, l, acc_ref in ((qt1, m1, l1, acc1_ref), (qt2, m2, l2, acc2_ref)):
            s = _dot(kb, qm)
            if masked:
                s = jnp.where(keep, s, MASK_VALUE)
            m_new = jnp.maximum(m, jnp.max(s, axis=0, keepdims=True))
            alpha = jnp.exp(m - m_new)
            p = jnp.exp(s - m_new)
            l_new = alpha * l + jnp.sum(p, axis=0, keepdims=True)
            acc_ref[...] = alpha * acc_ref[...] + _dot(vtb, p.astype(BF16))
            out += [m_new, l_new]
        return tuple(out)

    init = (jnp.full((1, tq), MASK_VALUE, F32), jnp.zeros((1, tq), F32)) * 2
    carry = lax.fori_loop(0, qi, functools.partial(block, masked=False), init)
    _, l1, _, l2 = block(qi, carry, True)

    ot = acc1_ref[...] / l1 - lam * (acc2_ref[...] / l2)
    o = ot.T
    o_ref[...] = (_rms(o, gain_ref[...]) * (1.0 - lam_init)).astype(o_ref.dtype)


def _attn(lq1, lk1, lq2, lk2, gain, qt, k, vt, *, batch, seq, lam_init):
    n, att_w = k.shape
    heads = att_w // V_DIM
    tq = ATT_TQ
    nq = seq // tq
    full = lambda a: pl.BlockSpec(a.shape, lambda b, h, i: (0,) * a.ndim)
    return pl.pallas_call(
        functools.partial(_attn_kernel, lam_init=lam_init),
        grid=(batch, heads, nq),
        in_specs=[full(lq1), full(lk1), full(lq2), full(lk2), full(gain),
                  pl.BlockSpec((None, V_DIM, tq), lambda b, h, i: (b, h, i)),
                  pl.BlockSpec((seq, V_DIM), lambda b, h, i: (b, h)),
                  pl.BlockSpec((None, V_DIM, seq), lambda b, h, i: (b, h, 0))],
        out_specs=pl.BlockSpec((tq, V_DIM), lambda b, h, i: (b * nq + i, h)),
        out_shape=jax.ShapeDtypeStruct((n, att_w), BF16),
        scratch_shapes=[pltpu.VMEM((V_DIM, tq), F32), pltpu.VMEM((V_DIM, tq), F32)],
        compiler_params=pltpu.CompilerParams(dimension_semantics=("arbitrary",) * 3,
                                             vmem_limit_bytes=VMEM_LIMIT),
        name="attn",
    )(lq1, lk1, lq2, lk2, gain, qt, k, vt)


def _mix_out_kernel(x_ref, ylru_ref, yatt_ref, wo_ref, g_post_ref, g_pre2_ref, wu_ref, wd_ref, g_post2_ref,
                    o_ref, acc_ref):
    lru_w = ylru_ref.shape[1]
    mixed = _dot(ylru_ref[...], wo_ref[0:lru_w, :]) + _dot(yatt_ref[...], wo_ref[lru_w:, :])
    h = x_ref[...] + _rms(mixed, g_post_ref[...])
    hm = _rms(h, g_pre2_ref[...]).astype(BF16)
    d_ff = wu_ref.shape[1]
    for c in range(d_ff // FF_CHUNK):
        sl = slice(c * FF_CHUNK, (c + 1) * FF_CHUNK)
        act = jnp.square(jnp.maximum(_dot(hm, wu_ref[:, sl]), 0.0)).astype(BF16)
        part = _dot(act, wd_ref[sl, :])
        if c == 0:
            acc_ref[...] = part
        else:
            acc_ref[...] += part
    o_ref[...] = h + _rms(acc_ref[...], g_post2_ref[...])


def _mix_out(x2, ylru, yatt, w_out, g_post, g_pre2, w_up, w_down, g_post2):
    n, d = x2.shape
    tm = ROW_TILE
    row_blk = lambda w: pl.BlockSpec((tm, w), lambda i: (i, 0))
    vec = lambda a: pl.BlockSpec(a.shape, lambda i: (0, 0))
    resident = lambda a: pl.BlockSpec(a.shape, lambda i: (0, 0), pipeline_mode=pl.Buffered(1))
    return pl.pallas_call(
        _mix_out_kernel,
        grid=(n // tm,),
        in_specs=[row_blk(d), row_blk(ylru.shape[1]), row_blk(yatt.shape[1]), resident(w_out), vec(g_post),
                  vec(g_pre2), resident(w_up), resident(w_down), vec(g_post2)],
        out_specs=row_blk(d),
        out_shape=jax.ShapeDtypeStruct((n, d), F32),
        scratch_shapes=[pltpu.VMEM((tm, d), F32)],
        compiler_params=pltpu.CompilerParams(dimension_semantics=("arbitrary",),
                                             vmem_limit_bytes=VMEM_LIMIT),
        name="mix_out",
    )(x2, ylru, yatt, w_out, g_post, g_pre2, w_up, w_down, g_post2)


def _rope_tables(seq):
    half = HEAD_DIM // 2
    freqs = ROPE_THETA ** (-jnp.arange(half, dtype=F32) / half)
    ang = jnp.arange(seq, dtype=F32)[:, None] * freqs[None, :]
    cos, sin = jnp.cos(ang), jnp.sin(ang)
    zero = jnp.zeros_like(sin)
    reps = V_DIM // HEAD_DIM
    cos_t = jnp.tile(jnp.concatenate([cos, cos], axis=1), (1, reps))
    sin_lo = jnp.tile(jnp.concatenate([-sin, zero], axis=1), (1, reps))
    sin_hi = jnp.tile(jnp.concatenate([zero, sin], axis=1), (1, reps))
    return cos_t, sin_lo, sin_hi


def _block_diag(w):
    nb, bs, _ = w.shape
    eye = jnp.eye(nb, dtype=w.dtype)
    return (eye[:, None, :, None] * w[:, :, None, :]).reshape(nb * bs, nb * bs)


def kernel(x, norm_mix_pre, w_in, conv_w, conv_b, w_gate_a, b_gate_a, w_gate_x, b_gate_x, lru_lambda, lambda_q1, lambda_k1, lambda_q2, lambda_k2, att_head_norm, w_out, norm_mix_post, norm_mlp_pre, w_up, w_down, norm_mlp_post):
    batch, seq, d = x.shape
    depth = w_in.shape[0]
    cos, sin_lo, sin_hi = _rope_tables(seq)
    row = lambda a: a.reshape(1, -1)
    x2 = x.reshape(batch * seq, d)
    for l in range(depth):
        lam_init = 0.8 - 0.6 * math.exp(-0.3 * l)
        ylru, q, k, v = _mix_in(
            x2, row(norm_mix_pre[l]), w_in[l].astype(BF16), conv_w[l], row(conv_b[l]),
            _block_diag(w_gate_a[l]).astype(BF16), row(b_gate_a[l]),
            _block_diag(w_gate_x[l]).astype(BF16), row(b_gate_x[l]), row(lru_lambda[l]),
            cos, sin_lo, sin_hi, batch=batch, seq=seq)
        att_w = q.shape[1]
        qt = jnp.swapaxes(q.reshape(batch, seq, att_w), 1, 2)
        vt = jnp.swapaxes(v.reshape(batch, seq, att_w), 1, 2)
        yatt = _attn(row(lambda_q1[l]), row(lambda_k1[l]), row(lambda_q2[l]), row(lambda_k2[l]),
                     row(att_head_norm[l]), qt, k, vt, batch=batch, seq=seq, lam_init=lam_init)
        x2 = _mix_out(x2, ylru, yatt, w_out[l].astype(BF16), row(norm_mix_post[l]), row(norm_mlp_pre[l]),
                      w_up[l].astype(BF16), w_down[l].astype(BF16), row(norm_mlp_post[l]))
    return x2.reshape(batch, seq, d)
```

```python
import functools
import math

import jax
import jax.numpy as jnp
from jax import lax
from jax.experimental import pallas as pl
from jax.experimental.pallas import tpu as pltpu

F32 = jnp.float32
BF16 = jnp.bfloat16

NORM_EPS = 1e-6
LRU_C = 8.0
CONV_WIDTH = 4
LRU_BLOCKS = 8
HEAD_DIM = 64
V_DIM = 2 * HEAD_DIM
ROPE_THETA = 10000.0
MASK_VALUE = -1e30
LOG2_E = math.log2(math.e)
ONES_ROWS = 16

SUBLANES = 8
ROW_TILE = 512
ATT_TQ = 512
ATT_KC = 128
FF_CHUNK = 1024
VMEM_LIMIT = 56 * 1024 * 1024


def _rms(x, g):
    ms = jnp.mean(x * x, axis=-1, keepdims=True)
    return x * lax.rsqrt(ms + NORM_EPS) * g


def _dot(a, b):
    return jnp.dot(a, b, preferred_element_type=F32)


def _mix_in_kernel(x_ref, g_ref, w_ref, cw_ref, cb_ref, wa_ref, ba_ref, wx_ref, bx_ref, lam_ref,
                   cos_ref, sin_lo_ref, sin_hi_ref,
                   ylru_ref, q_ref, k_ref, v_ref,
                   uext_ref, a_ref, b_ref, hcar_ref, *, lru_w, att_w):
    t = pl.program_id(1)
    tm = x_ref.shape[0]

    @pl.when(t == 0)
    def _():
        uext_ref[0:SUBLANES, :] = jnp.zeros((SUBLANES, lru_w), F32)
        hcar_ref[...] = jnp.zeros_like(hcar_ref)

    hn = _rms(x_ref[...], g_ref[...]).astype(BF16)

    c_q, c_k, c_v = 2 * lru_w, 2 * lru_w + att_w, 2 * lru_w + 2 * att_w
    cos, sin_lo, sin_hi = cos_ref[...], sin_lo_ref[...], sin_hi_ref[...]

    def rope(th):
        up = pltpu.roll(th, V_DIM - HEAD_DIM // 2, 1)
        dn = pltpu.roll(th, HEAD_DIM // 2, 1)
        return th * cos + up * sin_lo + dn * sin_hi

    qf = _dot(hn, w_ref[:, c_q:c_k]) * (HEAD_DIM ** -0.5 * LOG2_E)
    kf = _dot(hn, w_ref[:, c_k:c_v])
    for h in range(att_w // V_DIM):
        sl = slice(h * V_DIM, (h + 1) * V_DIM)
        q_ref[:, sl] = rope(qf[:, sl]).astype(BF16)
        k_ref[:, sl] = rope(kf[:, sl]).astype(BF16)
    v_ref[...] = _dot(hn, w_ref[:, c_v:c_v + att_w]).astype(BF16)

    u = _dot(hn, w_ref[:, 0:lru_w])
    uext_ref[SUBLANES:SUBLANES + tm, :] = u
    xc = cb_ref[...] + cw_ref[CONV_WIDTH - 1:CONV_WIDTH, :] * u
    for j in range(1, CONV_WIDTH):
        xc = xc + cw_ref[CONV_WIDTH - 1 - j:CONV_WIDTH - j, :] * uext_ref[SUBLANES - j:SUBLANES - j + tm, :]
    uext_ref[0:SUBLANES, :] = uext_ref[tm:tm + SUBLANES, :]

    xcb = xc.astype(BF16)
    r = jax.nn.sigmoid(_dot(xcb, wa_ref[...]) + ba_ref[...])
    i = jax.nn.sigmoid(_dot(xcb, wx_ref[...]) + bx_ref[...])
    neg_lam = -lam_ref[...]
    softplus = jnp.maximum(neg_lam, 0.0) + jnp.log1p(jnp.exp(-jnp.abs(neg_lam)))
    log_a = (-LRU_C) * r * softplus
    a_ref[...] = jnp.exp(log_a)
    th = jnp.tanh(log_a)
    b_ref[...] = jnp.sqrt(-2.0 * th / (1.0 - th)) * (i * xc)

    row = lax.broadcasted_iota(jnp.int32, (SUBLANES, lru_w), 0)
    steps = [(s, row >= s) for s in (1, 2, 4)]

    def group(gidx, hprev):
        r0 = pl.multiple_of(gidx * SUBLANES, SUBLANES)
        a8 = a_ref[pl.ds(r0, SUBLANES), :]
        b8 = b_ref[pl.ds(r0, SUBLANES), :]
        for s, m in steps:
            a_sh = jnp.where(m, pltpu.roll(a8, s, 0), 1.0)
            b_sh = jnp.where(m, pltpu.roll(b8, s, 0), 0.0)
            b8 = a8 * b_sh + b8
            a8 = a8 * a_sh
        h8 = a8 * hprev + b8
        b_ref[pl.ds(r0, SUBLANES), :] = h8
        return jnp.broadcast_to(h8[SUBLANES - 1:SUBLANES, :], (SUBLANES, lru_w))

    hcar_ref[...] = lax.fori_loop(0, tm // SUBLANES, group, hcar_ref[...], unroll=4)

    gate = _dot(hn, w_ref[:, lru_w:2 * lru_w])
    ylru_ref[...] = (b_ref[...] * jax.nn.gelu(gate)).astype(BF16)


def _mix_in(x2, g, w_in, conv_w, conv_b, wa, ba, wx, bx, lam, cos, sin_lo, sin_hi, *, batch, seq):
    n, d = x2.shape
    lru_w = conv_w.shape[1]
    att_w = (w_in.shape[1] - 2 * lru_w) // 3
    tm = ROW_TILE
    nt = seq // tm
    row_blk = lambda w: pl.BlockSpec((tm, w), lambda b, t: (b * nt + t, 0))
    full = lambda a: pl.BlockSpec(a.shape, lambda b, t: (0,) * a.ndim)
    tab = pl.BlockSpec((tm, V_DIM), lambda b, t: (t, 0))
    out = jax.ShapeDtypeStruct((n, lru_w), BF16)
    return pl.pallas_call(
        functools.partial(_mix_in_kernel, lru_w=lru_w, att_w=att_w),
        grid=(batch, nt),
        in_specs=[row_blk(d), full(g), full(w_in), full(conv_w), full(conv_b), full(wa), full(ba),
                  full(wx), full(bx), full(lam), tab, tab, tab],
        out_specs=[row_blk(lru_w), row_blk(att_w), row_blk(att_w), row_blk(att_w)],
        out_shape=[out, jax.ShapeDtypeStruct((n, att_w), BF16), jax.ShapeDtypeStruct((n, att_w), BF16),
                   jax.ShapeDtypeStruct((n, att_w), BF16)],
        scratch_shapes=[pltpu.VMEM((tm + SUBLANES, lru_w), F32), pltpu.VMEM((tm, lru_w), F32),
                        pltpu.VMEM((tm, lru_w), F32), pltpu.VMEM((SUBLANES, lru_w), F32)],
        compiler_params=pltpu.CompilerParams(dimension_semantics=("arbitrary", "arbitrary"),
                                             vmem_limit_bytes=VMEM_LIMIT),
        name="mix_in",
    )(x2, g, w_in, conv_w, conv_b, wa, ba, wx, bx, lam, cos, sin_lo, sin_hi)


def _attn_kernel(lq1_ref, lk1_ref, lq2_ref, lk2_ref, gain_ref, qt_ref, k_ref, vt_ref, o_ref,
                 s_ref, p_ref, acc_ref, *, lam_init):
    qi = pl.program_id(2)
    tq = qt_ref.shape[1]
    tk = tq

    lam = (jnp.exp(jnp.sum(lq1_ref[...] * lk1_ref[...], axis=-1, keepdims=True))
           - jnp.exp(jnp.sum(lq2_ref[...] * lk2_ref[...], axis=-1, keepdims=True)) + lam_init)

    qt = qt_ref[...]
    drow = lax.broadcasted_iota(jnp.int32, qt.shape, 0)
    zero = jnp.zeros_like(qt)
    qt1 = jnp.where(drow < HEAD_DIM, qt, zero)
    qt2 = jnp.where(drow >= HEAD_DIM, qt, zero)

    acc_ref[...] = jnp.zeros_like(acc_ref)
    qms = (qt1, qt2)
    n_chunks = tk // ATT_KC

    def scores(mp, k0, masked):
        mx8 = None
        for c in range(n_chunks):
            rows = slice(c * ATT_KC, (c + 1) * ATT_KC)
            s = _dot(k_ref[pl.ds(k0 + c * ATT_KC, ATT_KC), :], qms[mp])
            if masked:
                kpos = lax.broadcasted_iota(jnp.int32, (ATT_KC, tq), 0) + c * ATT_KC
                qpos = lax.broadcasted_iota(jnp.int32, (ATT_KC, tq), 1)
                s = jnp.where(kpos <= qpos, s, MASK_VALUE)
            s_ref[mp, rows, :] = s
            part = jnp.max(s.reshape(ATT_KC // SUBLANES, SUBLANES, tq), axis=0)
            mx8 = part if mx8 is None else jnp.maximum(mx8, part)
        return jnp.max(mx8, axis=0, keepdims=True)

    def probs(mp, m_new):
        for c in range(n_chunks):
            rows = slice(c * ATT_KC, (c + 1) * ATT_KC)
            p_ref[mp, rows, :] = jnp.exp2((s_ref[mp, rows, :] - m_new).astype(BF16))

    def accumulate(mp, k0, alpha):
        pv = _dot(vt_ref[:, pl.ds(k0, tk)], p_ref[mp])
        acc_ref[mp] = alpha * acc_ref[mp] + pv

    def block(j, carry, masked):
        k0 = pl.multiple_of(j * tk, tk)
        m_new = [jnp.maximum(carry[mp], scores(mp, k0, masked)) for mp in range(2)]
        for mp in range(2):
            probs(mp, m_new[mp])
            accumulate(mp, k0, jnp.exp2(carry[mp] - m_new[mp]))
        return tuple(m_new)

    init = (jnp.full((1, tq), MASK_VALUE, F32),) * 2
    carry = lax.fori_loop(0, qi, functools.partial(block, masked=False), init)
    block(qi, carry, True)

    l1 = acc_ref[0, V_DIM:V_DIM + 1, :]
    l2 = acc_ref[1, V_DIM:V_DIM + 1, :]
    ot = acc_ref[0, 0:V_DIM, :] / l1 - lam * (acc_ref[1, 0:V_DIM, :] / l2)
    o = ot.T
    o_ref[...] = (_rms(o, gain_ref[...]) * (1.0 - lam_init)).astype(o_ref.dtype)


def _attn(lq1, lk1, lq2, lk2, gain, qt, k, vt, *, batch, seq, lam_init):
    n, att_w = k.shape
    heads = att_w // V_DIM
    tq = ATT_TQ
    nq = seq // tq
    full = lambda a: pl.BlockSpec(a.shape, lambda b, h, i: (0,) * a.ndim)
    return pl.pallas_call(
        functools.partial(_attn_kernel, lam_init=lam_init),
        grid=(batch, heads, nq),
        in_specs=[full(lq1), full(lk1), full(lq2), full(lk2), full(gain),
                  pl.BlockSpec((None, V_DIM, tq), lambda b, h, i: (b, h, i)),
                  pl.BlockSpec((seq, V_DIM), lambda b, h, i: (b, h)),
                  pl.BlockSpec((None, V_DIM + ONES_ROWS, seq), lambda b, h, i: (b, h, 0))],
        out_specs=pl.BlockSpec((tq, V_DIM), lambda b, h, i: (b * nq + i, h)),
        out_shape=jax.ShapeDtypeStruct((n, att_w), BF16),
        scratch_shapes=[pltpu.VMEM((2, tq, tq), F32), pltpu.VMEM((2, tq, tq), BF16),
                        pltpu.VMEM((2, V_DIM + ONES_ROWS, tq), F32)],
        compiler_params=pltpu.CompilerParams(dimension_semantics=("arbitrary",) * 3,
                                             vmem_limit_bytes=VMEM_LIMIT),
        name="attn",
    )(lq1, lk1, lq2, lk2, gain, qt, k, vt)


def _mix_out_kernel(x_ref, ylru_ref, yatt_ref, wo_ref, g_post_ref, g_pre2_ref, wu_ref, wd_ref, g_post2_ref,
                    o_ref, acc_ref):
    lru_w = ylru_ref.shape[1]
    mixed = _dot(ylru_ref[...], wo_ref[0:lru_w, :]) + _dot(yatt_ref[...], wo_ref[lru_w:, :])
    h = x_ref[...] + _rms(mixed, g_post_ref[...])
    hm = _rms(h, g_pre2_ref[...]).astype(BF16)
    d_ff = wu_ref.shape[1]
    for c in range(d_ff // FF_CHUNK):
        sl = slice(c * FF_CHUNK, (c + 1) * FF_CHUNK)
        act = jnp.square(jnp.maximum(_dot(hm, wu_ref[:, sl]), 0.0)).astype(BF16)
        part = _dot(act, wd_ref[sl, :])
        if c == 0:
            acc_ref[...] = part
        else:
            acc_ref[...] += part
    o_ref[...] = h + _rms(acc_ref[...], g_post2_ref[...])


def _mix_out(x2, ylru, yatt, w_out, g_post, g_pre2, w_up, w_down, g_post2):
    n, d = x2.shape
    tm = ROW_TILE
    row_blk = lambda w: pl.BlockSpec((tm, w), lambda i: (i, 0))
    vec = lambda a: pl.BlockSpec(a.shape, lambda i: (0, 0))
    resident = lambda a: pl.BlockSpec(a.shape, lambda i: (0, 0), pipeline_mode=pl.Buffered(1))
    return pl.pallas_call(
        _mix_out_kernel,
        grid=(n // tm,),
        in_specs=[row_blk(d), row_blk(ylru.shape[1]), row_blk(yatt.shape[1]), resident(w_out), vec(g_post),
                  vec(g_pre2), resident(w_up), resident(w_down), vec(g_post2)],
        out_specs=row_blk(d),
        out_shape=jax.ShapeDtypeStruct((n, d), F32),
        scratch_shapes=[pltpu.VMEM((tm, d), F32)],
        compiler_params=pltpu.CompilerParams(dimension_semantics=("arbitrary",),
                                             vmem_limit_bytes=VMEM_LIMIT),
        name="mix_out",
    )(x2, ylru, yatt, w_out, g_post, g_pre2, w_up, w_down, g_post2)


def _rope_tables(seq):
    half = HEAD_DIM // 2
    freqs = ROPE_THETA ** (-jnp.arange(half, dtype=F32) / half)
    ang = jnp.arange(seq, dtype=F32)[:, None] * freqs[None, :]
    cos, sin = jnp.cos(ang), jnp.sin(ang)
    zero = jnp.zeros_like(sin)
    reps = V_DIM // HEAD_DIM
    cos_t = jnp.tile(jnp.concatenate([cos, cos], axis=1), (1, reps))
    sin_lo = jnp.tile(jnp.concatenate([-sin, zero], axis=1), (1, reps))
    sin_hi = jnp.tile(jnp.concatenate([zero, sin], axis=1), (1, reps))
    return cos_t, sin_lo, sin_hi


def _block_diag(w):
    nb, bs, _ = w.shape
    eye = jnp.eye(nb, dtype=w.dtype)
    return (eye[:, None, :, None] * w[:, :, None, :]).reshape(nb * bs, nb * bs)


def kernel(x, norm_mix_pre, w_in, conv_w, conv_b, w_gate_a, b_gate_a, w_gate_x, b_gate_x, lru_lambda, lambda_q1, lambda_k1, lambda_q2, lambda_k2, att_head_norm, w_out, norm_mix_post, norm_mlp_pre, w_up, w_down, norm_mlp_post):
    batch, seq, d = x.shape
    depth = w_in.shape[0]
    cos, sin_lo, sin_hi = _rope_tables(seq)
    row = lambda a: a.reshape(1, -1)
    x2 = x.reshape(batch * seq, d)
    for l in range(depth):
        lam_init = 0.8 - 0.6 * math.exp(-0.3 * l)
        ylru, q, k, v = _mix_in(
            x2, row(norm_mix_pre[l]), w_in[l].astype(BF16), conv_w[l], row(conv_b[l]),
            _block_diag(w_gate_a[l]).astype(BF16), row(b_gate_a[l]),
            _block_diag(w_gate_x[l]).astype(BF16), row(b_gate_x[l]), row(lru_lambda[l]),
            cos, sin_lo, sin_hi, batch=batch, seq=seq)
        att_w = q.shape[1]
        qt = jnp.swapaxes(q.reshape(batch, seq, att_w), 1, 2)
        heads = att_w // V_DIM
        vt = jnp.swapaxes(v.reshape(batch, seq, att_w), 1, 2).reshape(batch, heads, V_DIM, seq)
        vt = jnp.concatenate([vt, jnp.ones((batch, heads, ONES_ROWS, seq), BF16)], axis=2)
        vt = vt.reshape(batch, heads * (V_DIM + ONES_ROWS), seq)
        yatt = _attn(row(lambda_q1[l]), row(lambda_k1[l]), row(lambda_q2[l]), row(lambda_k2[l]),
                     row(att_head_norm[l]), qt, k, vt, batch=batch, seq=seq, lam_init=lam_init)
        x2 = _mix_out(x2, ylru, yatt, w_out[l].astype(BF16), row(norm_mix_post[l]), row(norm_mlp_pre[l]),
                      w_up[l].astype(BF16), w_down[l].astype(BF16), row(norm_mlp_post[l]))
    return x2.reshape(batch, seq, d)
```

```python
import functools
import math

import jax
import jax.numpy as jnp
from jax import lax
from jax.experimental import pallas as pl
from jax.experimental.pallas import tpu as pltpu

F32 = jnp.float32
BF16 = jnp.bfloat16

NORM_EPS = 1e-6
LRU_C = 8.0
CONV_WIDTH = 4
LRU_BLOCKS = 8
HEAD_DIM = 64
V_DIM = 2 * HEAD_DIM
ROPE_THETA = 10000.0
MASK_VALUE = -1e30
LOG2_E = math.log2(math.e)
ONES_ROWS = 16

SUBLANES = 8
ROW_TILE = 512
ATT_TQ = 512
ATT_KC = 128
FF_CHUNK = 1024
VMEM_LIMIT = 56 * 1024 * 1024


def _rms(x, g):
    ms = jnp.mean(x * x, axis=-1, keepdims=True)
    return x * lax.rsqrt(ms + NORM_EPS) * g


def _dot(a, b):
    return jnp.dot(a, b, preferred_element_type=F32)


def _mix_in_kernel(x_ref, g_ref, w_ref, cw_ref, cb_ref, wa_ref, ba_ref, wx_ref, bx_ref, lam_ref,
                   cos_ref, sin_lo_ref, sin_hi_ref,
                   ylru_ref, q_ref, k_ref, v_ref,
                   uext_ref, a_ref, b_ref, hcar_ref, *, lru_w, att_w):
    t = pl.program_id(1)
    tm = x_ref.shape[0]

    @pl.when(t == 0)
    def _():
        uext_ref[0:SUBLANES, :] = jnp.zeros((SUBLANES, lru_w), F32)
        hcar_ref[...] = jnp.zeros_like(hcar_ref)

    hn = _rms(x_ref[...], g_ref[...]).astype(BF16)

    c_q, c_k, c_v = 2 * lru_w, 2 * lru_w + att_w, 2 * lru_w + 2 * att_w
    cos, sin_lo, sin_hi = cos_ref[...], sin_lo_ref[...], sin_hi_ref[...]

    def rope(th):
        up = pltpu.roll(th, V_DIM - HEAD_DIM // 2, 1)
        dn = pltpu.roll(th, HEAD_DIM // 2, 1)
        return th * cos + up * sin_lo + dn * sin_hi

    qf = _dot(hn, w_ref[:, c_q:c_k]) * (HEAD_DIM ** -0.5 * LOG2_E)
    kf = _dot(hn, w_ref[:, c_k:c_v])
    vf = _dot(hn, w_ref[:, c_v:c_v + att_w])
    for h in range(att_w // V_DIM):
        sl = slice(h * V_DIM, (h + 1) * V_DIM)
        q_ref[h] = rope(qf[:, sl]).astype(BF16)
        k_ref[h] = rope(kf[:, sl]).astype(BF16)
        v_ref[h] = vf[:, sl].astype(BF16)

    u = _dot(hn, w_ref[:, 0:lru_w])
    uext_ref[SUBLANES:SUBLANES + tm, :] = u
    xc = cb_ref[...] + cw_ref[CONV_WIDTH - 1:CONV_WIDTH, :] * u
    for j in range(1, CONV_WIDTH):
        xc = xc + cw_ref[CONV_WIDTH - 1 - j:CONV_WIDTH - j, :] * uext_ref[SUBLANES - j:SUBLANES - j + tm, :]
    uext_ref[0:SUBLANES, :] = uext_ref[tm:tm + SUBLANES, :]

    xcb = xc.astype(BF16)
    r = jax.nn.sigmoid(_dot(xcb, wa_ref[...]) + ba_ref[...])
    i = jax.nn.sigmoid(_dot(xcb, wx_ref[...]) + bx_ref[...])
    neg_lam = -lam_ref[...]
    softplus = jnp.maximum(neg_lam, 0.0) + jnp.log1p(jnp.exp(-jnp.abs(neg_lam)))
    log_a = (-LRU_C) * r * softplus
    a_ref[...] = jnp.exp(log_a)
    th = jnp.tanh(log_a)
    b_ref[...] = jnp.sqrt(-2.0 * th / (1.0 - th)) * (i * xc)

    row = lax.broadcasted_iota(jnp.int32, (SUBLANES, lru_w), 0)
    steps = [(s, row >= s) for s in (1, 2, 4)]

    def group(gidx, hprev):
        r0 = pl.multiple_of(gidx * SUBLANES, SUBLANES)
        a8 = a_ref[pl.ds(r0, SUBLANES), :]
        b8 = b_ref[pl.ds(r0, SUBLANES), :]
        for s, m in steps:
            a_sh = jnp.where(m, pltpu.roll(a8, s, 0), 1.0)
            b_sh = jnp.where(m, pltpu.roll(b8, s, 0), 0.0)
            b8 = a8 * b_sh + b8
            a8 = a8 * a_sh
        h8 = a8 * hprev + b8
        b_ref[pl.ds(r0, SUBLANES), :] = h8
        return jnp.broadcast_to(h8[SUBLANES - 1:SUBLANES, :], (SUBLANES, lru_w))

    hcar_ref[...] = lax.fori_loop(0, tm // SUBLANES, group, hcar_ref[...], unroll=4)

    gate = _dot(hn, w_ref[:, lru_w:2 * lru_w])
    ylru_ref[...] = (b_ref[...] * jax.nn.gelu(gate)).astype(BF16)


def _mix_in(x2, g, w_in, conv_w, conv_b, wa, ba, wx, bx, lam, cos, sin_lo, sin_hi, *, batch, seq):
    n, d = x2.shape
    lru_w = conv_w.shape[1]
    att_w = (w_in.shape[1] - 2 * lru_w) // 3
    tm = ROW_TILE
    nt = seq // tm
    row_blk = lambda w: pl.BlockSpec((tm, w), lambda b, t: (b * nt + t, 0))
    full = lambda a: pl.BlockSpec(a.shape, lambda b, t: (0,) * a.ndim)
    tab = pl.BlockSpec((tm, V_DIM), lambda b, t: (t, 0))
    heads = att_w // V_DIM
    head_major = jax.ShapeDtypeStruct((batch, heads, seq, V_DIM), BF16)
    head_blk = pl.BlockSpec((None, heads, tm, V_DIM), lambda b, t: (b, 0, t, 0))
    return pl.pallas_call(
        functools.partial(_mix_in_kernel, lru_w=lru_w, att_w=att_w),
        grid=(batch, nt),
        in_specs=[row_blk(d), full(g), full(w_in), full(conv_w), full(conv_b), full(wa), full(ba),
                  full(wx), full(bx), full(lam), tab, tab, tab],
        out_specs=[row_blk(lru_w), head_blk, head_blk, head_blk],
        out_shape=[jax.ShapeDtypeStruct((n, lru_w), BF16), head_major, head_major, head_major],
        scratch_shapes=[pltpu.VMEM((tm + SUBLANES, lru_w), F32), pltpu.VMEM((tm, lru_w), F32),
                        pltpu.VMEM((tm, lru_w), F32), pltpu.VMEM((SUBLANES, lru_w), F32)],
        compiler_params=pltpu.CompilerParams(dimension_semantics=("arbitrary", "arbitrary"),
                                             vmem_limit_bytes=VMEM_LIMIT),
        name="mix_in",
    )(x2, g, w_in, conv_w, conv_b, wa, ba, wx, bx, lam, cos, sin_lo, sin_hi)


def _attn_kernel(lq1_ref, lk1_ref, lq2_ref, lk2_ref, gain_ref, q_ref, k_ref, v_ref, o_ref,
                 vt_ref, s_ref, p_ref, acc_ref, *, lam_init):
    qi = pl.program_id(2)
    tq = q_ref.shape[0]
    tk = tq
    seq = k_ref.shape[0]

    @pl.when(qi == 0)
    def _():
        vt_ref[V_DIM:, :] = jnp.ones((ONES_ROWS, seq), BF16)
        for c in range(seq // tk):
            cols = slice(c * tk, (c + 1) * tk)
            vt_ref[0:V_DIM, cols] = v_ref[cols, :].astype(F32).T.astype(BF16)

    lam = (jnp.exp(jnp.sum(lq1_ref[...] * lk1_ref[...], axis=-1, keepdims=True))
           - jnp.exp(jnp.sum(lq2_ref[...] * lk2_ref[...], axis=-1, keepdims=True)) + lam_init)

    qt = q_ref[...].astype(F32).T
    drow = lax.broadcasted_iota(jnp.int32, qt.shape, 0)
    zero = jnp.zeros_like(qt)
    qt1 = jnp.where(drow < HEAD_DIM, qt, zero).astype(BF16)
    qt2 = jnp.where(drow >= HEAD_DIM, qt, zero).astype(BF16)

    acc_ref[...] = jnp.zeros_like(acc_ref)
    qms = (qt1, qt2)
    n_chunks = tk // ATT_KC

    def scores(mp, k0, masked):
        mx8 = None
        for c in range(n_chunks):
            rows = slice(c * ATT_KC, (c + 1) * ATT_KC)
            s = _dot(k_ref[pl.ds(k0 + c * ATT_KC, ATT_KC), :], qms[mp])
            if masked:
                kpos = lax.broadcasted_iota(jnp.int32, (ATT_KC, tq), 0) + c * ATT_KC
                qpos = lax.broadcasted_iota(jnp.int32, (ATT_KC, tq), 1)
                s = jnp.where(kpos <= qpos, s, MASK_VALUE)
            s_ref[mp, rows, :] = s
            part = jnp.max(s.reshape(ATT_KC // SUBLANES, SUBLANES, tq), axis=0)
            mx8 = part if mx8 is None else jnp.maximum(mx8, part)
        return jnp.max(mx8, axis=0, keepdims=True)

    def probs(mp, m_new):
        for c in range(n_chunks):
            rows = slice(c * ATT_KC, (c + 1) * ATT_KC)
            p_ref[mp, rows, :] = jnp.exp2((s_ref[mp, rows, :] - m_new).astype(BF16))

    def accumulate(mp, k0, alpha):
        pv = _dot(vt_ref[:, pl.ds(k0, tk)], p_ref[mp])
        acc_ref[mp] = alpha * acc_ref[mp] + pv

    def block(j, carry, masked):
        k0 = pl.multiple_of(j * tk, tk)
        m_new = [jnp.maximum(carry[mp], scores(mp, k0, masked)) for mp in range(2)]
        for mp in range(2):
            probs(mp, m_new[mp])
            accumulate(mp, k0, jnp.exp2(carry[mp] - m_new[mp]))
        return tuple(m_new)

    init = (jnp.full((1, tq), MASK_VALUE, F32),) * 2
    carry = lax.fori_loop(0, qi, functools.partial(block, masked=False), init)
    block(qi, carry, True)

    l1 = acc_ref[0, V_DIM:V_DIM + 1, :]
    l2 = acc_ref[1, V_DIM:V_DIM + 1, :]
    ot = acc_ref[0, 0:V_DIM, :] / l1 - lam * (acc_ref[1, 0:V_DIM, :] / l2)
    o = ot.T
    o_ref[...] = (_rms(o, gain_ref[...]) * (1.0 - lam_init)).astype(o_ref.dtype)


def _attn(lq1, lk1, lq2, lk2, gain, q, k, v, *, lam_init):
    batch, heads, seq, _ = k.shape
    tq = ATT_TQ
    nq = seq // tq
    full = lambda a: pl.BlockSpec(a.shape, lambda b, h, i: (0,) * a.ndim)
    whole_seq = pl.BlockSpec((None, None, seq, V_DIM), lambda b, h, i: (b, h, 0, 0))
    return pl.pallas_call(
        functools.partial(_attn_kernel, lam_init=lam_init),
        grid=(batch, heads, nq),
        in_specs=[full(lq1), full(lk1), full(lq2), full(lk2), full(gain),
                  pl.BlockSpec((None, None, tq, V_DIM), lambda b, h, i: (b, h, i, 0)), whole_seq, whole_seq],
        out_specs=pl.BlockSpec((tq, V_DIM), lambda b, h, i: (b * nq + i, h)),
        out_shape=jax.ShapeDtypeStruct((batch * seq, heads * V_DIM), BF16),
        scratch_shapes=[pltpu.VMEM((V_DIM + ONES_ROWS, seq), BF16), pltpu.VMEM((2, tq, tq), F32),
                        pltpu.VMEM((2, tq, tq), BF16), pltpu.VMEM((2, V_DIM + ONES_ROWS, tq), F32)],
        compiler_params=pltpu.CompilerParams(dimension_semantics=("arbitrary",) * 3,
                                             vmem_limit_bytes=VMEM_LIMIT),
        name="attn",
    )(lq1, lk1, lq2, lk2, gain, q, k, v)


def _mix_out_kernel(x_ref, ylru_ref, yatt_ref, wo_ref, g_post_ref, g_pre2_ref, wu_ref, wd_ref, g_post2_ref,
                    o_ref, acc_ref):
    lru_w = ylru_ref.shape[1]
    mixed = _dot(ylru_ref[...], wo_ref[0:lru_w, :]) + _dot(yatt_ref[...], wo_ref[lru_w:, :])
    h = x_ref[...] + _rms(mixed, g_post_ref[...])
    hm = _rms(h, g_pre2_ref[...]).astype(BF16)
    d_ff = wu_ref.shape[1]
    for c in range(d_ff // FF_CHUNK):
        sl = slice(c * FF_CHUNK, (c + 1) * FF_CHUNK)
        act = jnp.square(jnp.maximum(_dot(hm, wu_ref[:, sl]), 0.0)).astype(BF16)
        part = _dot(act, wd_ref[sl, :])
        if c == 0:
            acc_ref[...] = part
        else:
            acc_ref[...] += part
    o_ref[...] = h + _rms(acc_ref[...], g_post2_ref[...])


def _mix_out(x2, ylru, yatt, w_out, g_post, g_pre2, w_up, w_down, g_post2):
    n, d = x2.shape
    tm = ROW_TILE
    row_blk = lambda w: pl.BlockSpec((tm, w), lambda i: (i, 0))
    vec = lambda a: pl.BlockSpec(a.shape, lambda i: (0, 0))
    resident = lambda a: pl.BlockSpec(a.shape, lambda i: (0, 0), pipeline_mode=pl.Buffered(1))
    return pl.pallas_call(
        _mix_out_kernel,
        grid=(n // tm,),
        in_specs=[row_blk(d), row_blk(ylru.shape[1]), row_blk(yatt.shape[1]), resident(w_out), vec(g_post),
                  vec(g_pre2), resident(w_up), resident(w_down), vec(g_post2)],
        out_specs=row_blk(d),
        out_shape=jax.ShapeDtypeStruct((n, d), F32),
        scratch_shapes=[pltpu.VMEM((tm, d), F32)],
        compiler_params=pltpu.CompilerParams(dimension_semantics=("arbitrary",),
                                             vmem_limit_bytes=VMEM_LIMIT),
        name="mix_out",
    )(x2, ylru, yatt, w_out, g_post, g_pre2, w_up, w_down, g_post2)


def _rope_tables(seq):
    half = HEAD_DIM // 2
    freqs = ROPE_THETA ** (-jnp.arange(half, dtype=F32) / half)
    ang = jnp.arange(seq, dtype=F32)[:, None] * freqs[None, :]
    cos, sin = jnp.cos(ang), jnp.sin(ang)
    zero = jnp.zeros_like(sin)
    reps = V_DIM // HEAD_DIM
    cos_t = jnp.tile(jnp.concatenate([cos, cos], axis=1), (1, reps))
    sin_lo = jnp.tile(jnp.concatenate([-sin, zero], axis=1), (1, reps))
    sin_hi = jnp.tile(jnp.concatenate([zero, sin], axis=1), (1, reps))
    return cos_t, sin_lo, sin_hi


def _block_diag(w):
    nb, bs, _ = w.shape
    eye = jnp.eye(nb, dtype=w.dtype)
    return (eye[:, None, :, None] * w[:, :, None, :]).reshape(nb * bs, nb * bs)


def kernel(x, norm_mix_pre, w_in, conv_w, conv_b, w_gate_a, b_gate_a, w_gate_x, b_gate_x, lru_lambda, lambda_q1, lambda_k1, lambda_q2, lambda_k2, att_head_norm, w_out, norm_mix_post, norm_mlp_pre, w_up, w_down, norm_mlp_post):
    batch, seq, d = x.shape
    depth = w_in.shape[0]
    cos, sin_lo, sin_hi = _rope_tables(seq)
    row = lambda a: a.reshape(1, -1)
    x2 = x.reshape(batch * seq, d)
    for l in range(depth):
        lam_init = 0.8 - 0.6 * math.exp(-0.3 * l)
        ylru, q, k, v = _mix_in(
            x2, row(norm_mix_pre[l]), w_in[l].astype(BF16), conv_w[l], row(conv_b[l]),
            _block_diag(w_gate_a[l]).astype(BF16), row(b_gate_a[l]),
            _block_diag(w_gate_x[l]).astype(BF16), row(b_gate_x[l]), row(lru_lambda[l]),
            cos, sin_lo, sin_hi, batch=batch, seq=seq)
        yatt = _attn(row(lambda_q1[l]), row(lambda_k1[l]), row(lambda_q2[l]), row(lambda_k2[l]),
                     row(att_head_norm[l]), q, k, v, lam_init=lam_init)
        x2 = _mix_out(x2, ylru, yatt, w_out[l].astype(BF16), row(norm_mix_post[l]), row(norm_mlp_pre[l]),
                      w_up[l].astype(BF16), w_down[l].astype(BF16), row(norm_mlp_post[l]))
    return x2.reshape(batch, seq, d)
```

```python
import functools
import math

import jax
import jax.numpy as jnp
from jax import lax
from jax.experimental import pallas as pl
from jax.experimental.pallas import tpu as pltpu

F32 = jnp.float32
BF16 = jnp.bfloat16

NORM_EPS = 1e-6
LRU_C = 8.0
CONV_WIDTH = 4
LRU_BLOCKS = 8
HEAD_DIM = 64
V_DIM = 2 * HEAD_DIM
ROPE_THETA = 10000.0
MASK_VALUE = -1e30
LOG2_E = math.log2(math.e)
ONES_ROWS = 16

SUBLANES = 8
ROW_TILE = 512
ATT_TQ = 512
ATT_KC = 128
FF_CHUNK = 1024
VMEM_LIMIT = 56 * 1024 * 1024


def _rms(x, g):
    ms = jnp.mean(x * x, axis=-1, keepdims=True)
    return x * lax.rsqrt(ms + NORM_EPS) * g


def _dot(a, b):
    return jnp.dot(a, b, preferred_element_type=F32)


def _mix_in_kernel(x_ref, g_ref, w_ref, cw_ref, cb_ref, wa_ref, ba_ref, wx_ref, bx_ref, lam_ref,
                   cos_ref, sin_lo_ref, sin_hi_ref,
                   ylru_ref, q_ref, k_ref, v_ref,
                   uext_ref, a_ref, b_ref, hcar_ref, *, lru_w, att_w):
    t = pl.program_id(1)
    tm = x_ref.shape[0]

    @pl.when(t == 0)
    def _():
        uext_ref[0:SUBLANES, :] = jnp.zeros((SUBLANES, lru_w), F32)
        hcar_ref[...] = jnp.zeros_like(hcar_ref)

    hn = _rms(x_ref[...], g_ref[...]).astype(BF16)

    c_q, c_k, c_v = 2 * lru_w, 2 * lru_w + att_w, 2 * lru_w + 2 * att_w
    cos, sin_lo, sin_hi = cos_ref[...], sin_lo_ref[...], sin_hi_ref[...]

    def rope(th):
        up = pltpu.roll(th, V_DIM - HEAD_DIM // 2, 1)
        dn = pltpu.roll(th, HEAD_DIM // 2, 1)
        return th * cos + up * sin_lo + dn * sin_hi

    qf = _dot(hn, w_ref[:, c_q:c_k]) * (HEAD_DIM ** -0.5 * LOG2_E)
    kf = _dot(hn, w_ref[:, c_k:c_v])
    vf = _dot(hn, w_ref[:, c_v:c_v + att_w])
    for h in range(att_w // V_DIM):
        sl = slice(h * V_DIM, (h + 1) * V_DIM)
        q_ref[h] = rope(qf[:, sl]).astype(BF16)
        k_ref[h] = rope(kf[:, sl]).astype(BF16)
        v_ref[h] = vf[:, sl].astype(BF16)

    u = _dot(hn, w_ref[:, 0:lru_w])
    uext_ref[SUBLANES:SUBLANES + tm, :] = u
    xc = cb_ref[...] + cw_ref[CONV_WIDTH - 1:CONV_WIDTH, :] * u
    for j in range(1, CONV_WIDTH):
        xc = xc + cw_ref[CONV_WIDTH - 1 - j:CONV_WIDTH - j, :] * uext_ref[SUBLANES - j:SUBLANES - j + tm, :]
    uext_ref[0:SUBLANES, :] = uext_ref[tm:tm + SUBLANES, :]

    xcb = xc.astype(BF16)
    r = jax.nn.sigmoid(_dot(xcb, wa_ref[...]) + ba_ref[...])
    i = jax.nn.sigmoid(_dot(xcb, wx_ref[...]) + bx_ref[...])
    neg_lam = -lam_ref[...]
    softplus = jnp.maximum(neg_lam, 0.0) + jnp.log1p(jnp.exp(-jnp.abs(neg_lam)))
    log_a = (-LRU_C) * r * softplus
    a_ref[...] = jnp.exp(log_a)
    th = jnp.tanh(log_a)
    b_ref[...] = jnp.sqrt(-2.0 * th / (1.0 - th)) * (i * xc)

    row = lax.broadcasted_iota(jnp.int32, (SUBLANES, lru_w), 0)
    steps = [(s, row >= s) for s in (1, 2, 4)]

    def group(gidx, hprev):
        r0 = pl.multiple_of(gidx * SUBLANES, SUBLANES)
        a8 = a_ref[pl.ds(r0, SUBLANES), :]
        b8 = b_ref[pl.ds(r0, SUBLANES), :]
        for s, m in steps:
            a_sh = jnp.where(m, pltpu.roll(a8, s, 0), 1.0)
            b_sh = jnp.where(m, pltpu.roll(b8, s, 0), 0.0)
            b8 = a8 * b_sh + b8
            a8 = a8 * a_sh
        h8 = a8 * hprev + b8
        b_ref[pl.ds(r0, SUBLANES), :] = h8
        return jnp.broadcast_to(h8[SUBLANES - 1:SUBLANES, :], (SUBLANES, lru_w))

    hcar_ref[...] = lax.fori_loop(0, tm // SUBLANES, group, hcar_ref[...], unroll=4)

    gate = _dot(hn, w_ref[:, lru_w:2 * lru_w])
    ylru_ref[...] = (b_ref[...] * jax.nn.gelu(gate)).astype(BF16)


def _mix_in(x2, g, w_in, conv_w, conv_b, wa, ba, wx, bx, lam, cos, sin_lo, sin_hi, *, batch, seq):
    n, d = x2.shape
    lru_w = conv_w.shape[1]
    att_w = (w_in.shape[1] - 2 * lru_w) // 3
    tm = ROW_TILE
    nt = seq // tm
    row_blk = lambda w: pl.BlockSpec((tm, w), lambda b, t: (b * nt + t, 0))
    full = lambda a: pl.BlockSpec(a.shape, lambda b, t: (0,) * a.ndim)
    tab = pl.BlockSpec((tm, V_DIM), lambda b, t: (t, 0))
    heads = att_w // V_DIM
    head_major = jax.ShapeDtypeStruct((batch, heads, seq, V_DIM), BF16)
    head_blk = pl.BlockSpec((None, heads, tm, V_DIM), lambda b, t: (b, 0, t, 0))
    return pl.pallas_call(
        functools.partial(_mix_in_kernel, lru_w=lru_w, att_w=att_w),
        grid=(batch, nt),
        in_specs=[row_blk(d), full(g), full(w_in), full(conv_w), full(conv_b), full(wa), full(ba),
                  full(wx), full(bx), full(lam), tab, tab, tab],
        out_specs=[row_blk(lru_w), head_blk, head_blk, head_blk],
        out_shape=[jax.ShapeDtypeStruct((n, lru_w), BF16), head_major, head_major, head_major],
        scratch_shapes=[pltpu.VMEM((tm + SUBLANES, lru_w), F32), pltpu.VMEM((tm, lru_w), F32),
                        pltpu.VMEM((tm, lru_w), F32), pltpu.VMEM((SUBLANES, lru_w), F32)],
        compiler_params=pltpu.CompilerParams(dimension_semantics=("arbitrary", "arbitrary"),
                                             vmem_limit_bytes=VMEM_LIMIT),
        name="mix_in",
    )(x2, g, w_in, conv_w, conv_b, wa, ba, wx, bx, lam, cos, sin_lo, sin_hi)


def _attn_kernel(lq1_ref, lk1_ref, lq2_ref, lk2_ref, gain_ref, q_ref, k_ref, v_ref, o_ref,
                 vt_ref, qm_ref, s_ref, p_ref, acc_ref, *, lam_init):
    seq = k_ref.shape[0]
    tq = tk = ATT_TQ
    nq = seq // tq
    n_chunks = tk // ATT_KC

    vt_ref[V_DIM:, :] = jnp.ones((ONES_ROWS, seq), BF16)
    for c in range(seq // tk):
        cols = slice(c * tk, (c + 1) * tk)
        vt_ref[0:V_DIM, cols] = v_ref[cols, :].astype(F32).T.astype(BF16)

    lam = (jnp.exp(jnp.sum(lq1_ref[...] * lk1_ref[...], axis=-1, keepdims=True))
           - jnp.exp(jnp.sum(lq2_ref[...] * lk2_ref[...], axis=-1, keepdims=True)) + lam_init)

    def load_queries(i):
        qt = q_ref[i * tq:(i + 1) * tq, :].astype(F32).T
        drow = lax.broadcasted_iota(jnp.int32, qt.shape, 0)
        zero = jnp.zeros_like(qt)
        qm_ref[i % 2, 0] = jnp.where(drow < HEAD_DIM, qt, zero).astype(BF16)
        qm_ref[i % 2, 1] = jnp.where(drow >= HEAD_DIM, qt, zero).astype(BF16)

    def stage(score_unit, prob_unit, m_old, mx):
        if prob_unit is not None:
            pi, pj, pm = prob_unit
            m_new = mx if m_old is None else jnp.maximum(m_old, mx)
        mx8 = None
        for c in range(n_chunks):
            rows = slice(c * ATT_KC, (c + 1) * ATT_KC)
            if score_unit is not None:
                si, sj, sm = score_unit
                k0 = sj * tk + c * ATT_KC
                s = _dot(k_ref[k0:k0 + ATT_KC, :], qm_ref[si % 2, sm])
                if sj == si:
                    kpos = lax.broadcasted_iota(jnp.int32, (ATT_KC, tq), 0) + c * ATT_KC
                    qpos = lax.broadcasted_iota(jnp.int32, (ATT_KC, tq), 1)
                    s = jnp.where(kpos <= qpos, s, MASK_VALUE)
                s_ref[sm, rows, :] = s
                part = jnp.max(s.reshape(ATT_KC // SUBLANES, SUBLANES, tq), axis=0)
                mx8 = part if mx8 is None else jnp.maximum(mx8, part)
            if prob_unit is not None:
                p_ref[pm, rows, :] = jnp.exp2(s_ref[pm, rows, :] - m_new).astype(BF16)
        if prob_unit is not None:
            pv = _dot(vt_ref[:, pj * tk:(pj + 1) * tk], p_ref[pm])
            if m_old is None:
                acc_ref[pi % 2, pm] = pv
            else:
                acc_ref[pi % 2, pm] = jnp.exp2(m_old - m_new) * acc_ref[pi % 2, pm] + pv
        new_mx = None if score_unit is None else jnp.max(mx8, axis=0, keepdims=True)
        return new_mx, (m_new if prob_unit is not None else None)

    def finish(i):
        acc1, acc2 = acc_ref.at[i % 2, 0], acc_ref.at[i % 2, 1]
        ot = (acc1[0:V_DIM, :] / acc1[V_DIM:V_DIM + 1, :]
              - lam * (acc2[0:V_DIM, :] / acc2[V_DIM:V_DIM + 1, :]))
        o = ot.T
        o_ref[i * tq:(i + 1) * tq, :] = (_rms(o, gain_ref[...]) * (1.0 - lam_init)).astype(o_ref.dtype)

    units = [(i, j, mp) for i in range(nq) for j in range(i + 1) for mp in range(2)]
    load_queries(0)
    mx, _ = stage(units[0], None, None, None)
    m_run = {}
    for t, unit in enumerate(units):
        i, j, mp = unit
        nxt = units[t + 1] if t + 1 < len(units) else None
        if nxt is not None and nxt[0] != i:
            load_queries(nxt[0])
        mx, m_run[(i, mp)] = stage(nxt, unit, m_run.get((i, mp)), mx)
        if j == i and mp == 1:
            finish(i)


def _attn(lq1, lk1, lq2, lk2, gain, q, k, v, *, lam_init):
    batch, heads, seq, _ = k.shape
    tq = ATT_TQ
    full = lambda a: pl.BlockSpec(a.shape, lambda b, h: (0,) * a.ndim)
    whole_seq = pl.BlockSpec((None, None, seq, V_DIM), lambda b, h: (b, h, 0, 0))
    return pl.pallas_call(
        functools.partial(_attn_kernel, lam_init=lam_init),
        grid=(batch, heads),
        in_specs=[full(lq1), full(lk1), full(lq2), full(lk2), full(gain), whole_seq, whole_seq, whole_seq],
        out_specs=pl.BlockSpec((seq, V_DIM), lambda b, h: (b, h)),
        out_shape=jax.ShapeDtypeStruct((batch * seq, heads * V_DIM), BF16),
        scratch_shapes=[pltpu.VMEM((V_DIM + ONES_ROWS, seq), BF16), pltpu.VMEM((2, 2, V_DIM, tq), BF16),
                        pltpu.VMEM((2, tq, tq), F32), pltpu.VMEM((2, tq, tq), BF16),
                        pltpu.VMEM((2, 2, V_DIM + ONES_ROWS, tq), F32)],
        compiler_params=pltpu.CompilerParams(dimension_semantics=("arbitrary",) * 2,
                                             vmem_limit_bytes=VMEM_LIMIT),
        name="attn",
    )(lq1, lk1, lq2, lk2, gain, q, k, v)


def _mix_out_kernel(x_ref, ylru_ref, yatt_ref, wo_ref, g_post_ref, g_pre2_ref, wu_ref, wd_ref, g_post2_ref,
                    o_ref, acc_ref):
    lru_w = ylru_ref.shape[1]
    mixed = _dot(ylru_ref[...], wo_ref[0:lru_w, :]) + _dot(yatt_ref[...], wo_ref[lru_w:, :])
    h = x_ref[...] + _rms(mixed, g_post_ref[...])
    hm = _rms(h, g_pre2_ref[...]).astype(BF16)
    d_ff = wu_ref.shape[1]
    for c in range(d_ff // FF_CHUNK):
        sl = slice(c * FF_CHUNK, (c + 1) * FF_CHUNK)
        act = jnp.square(jnp.maximum(_dot(hm, wu_ref[:, sl]), 0.0)).astype(BF16)
        part = _dot(act, wd_ref[sl, :])
        if c == 0:
            acc_ref[...] = part
        else:
            acc_ref[...] += part
    o_ref[...] = h + _rms(acc_ref[...], g_post2_ref[...])


def _mix_out(x2, ylru, yatt, w_out, g_post, g_pre2, w_up, w_down, g_post2):
    n, d = x2.shape
    tm = ROW_TILE
    row_blk = lambda w: pl.BlockSpec((tm, w), lambda i: (i, 0))
    vec = lambda a: pl.BlockSpec(a.shape, lambda i: (0, 0))
    resident = lambda a: pl.BlockSpec(a.shape, lambda i: (0, 0), pipeline_mode=pl.Buffered(1))
    return pl.pallas_call(
        _mix_out_kernel,
        grid=(n // tm,),
        in_specs=[row_blk(d), row_blk(ylru.shape[1]), row_blk(yatt.shape[1]), resident(w_out), vec(g_post),
                  vec(g_pre2), resident(w_up), resident(w_down), vec(g_post2)],
        out_specs=row_blk(d),
        out_shape=jax.ShapeDtypeStruct((n, d), F32),
        scratch_shapes=[pltpu.VMEM((tm, d), F32)],
        compiler_params=pltpu.CompilerParams(dimension_semantics=("arbitrary",),
                                             vmem_limit_bytes=VMEM_LIMIT),
        name="mix_out",
    )(x2, ylru, yatt, w_out, g_post, g_pre2, w_up, w_down, g_post2)


def _rope_tables(seq):
    half = HEAD_DIM // 2
    freqs = ROPE_THETA ** (-jnp.arange(half, dtype=F32) / half)
    ang = jnp.arange(seq, dtype=F32)[:, None] * freqs[None, :]
    cos, sin = jnp.cos(ang), jnp.sin(ang)
    zero = jnp.zeros_like(sin)
    reps = V_DIM // HEAD_DIM
    cos_t = jnp.tile(jnp.concatenate([cos, cos], axis=1), (1, reps))
    sin_lo = jnp.tile(jnp.concatenate([-sin, zero], axis=1), (1, reps))
    sin_hi = jnp.tile(jnp.concatenate([zero, sin], axis=1), (1, reps))
    return cos_t, sin_lo, sin_hi


def _block_diag(w):
    nb, bs, _ = w.shape
    eye = jnp.eye(nb, dtype=w.dtype)
    return (eye[:, None, :, None] * w[:, :, None, :]).reshape(nb * bs, nb * bs)


def kernel(x, norm_mix_pre, w_in, conv_w, conv_b, w_gate_a, b_gate_a, w_gate_x, b_gate_x, lru_lambda, lambda_q1, lambda_k1, lambda_q2, lambda_k2, att_head_norm, w_out, norm_mix_post, norm_mlp_pre, w_up, w_down, norm_mlp_post):
    batch, seq, d = x.shape
    depth = w_in.shape[0]
    cos, sin_lo, sin_hi = _rope_tables(seq)
    row = lambda a: a.reshape(1, -1)
    x2 = x.reshape(batch * seq, d)
    for l in range(depth):
        lam_init = 0.8 - 0.6 * math.exp(-0.3 * l)
        ylru, q, k, v = _mix_in(
            x2, row(norm_mix_pre[l]), w_in[l].astype(BF16), conv_w[l], row(conv_b[l]),
            _block_diag(w_gate_a[l]).astype(BF16), row(b_gate_a[l]),
            _block_diag(w_gate_x[l]).astype(BF16), row(b_gate_x[l]), row(lru_lambda[l]),
            cos, sin_lo, sin_hi, batch=batch, seq=seq)
        yatt = _attn(row(lambda_q1[l]), row(lambda_k1[l]), row(lambda_q2[l]), row(lambda_k2[l]),
                     row(att_head_norm[l]), q, k, v, lam_init=lam_init)
        x2 = _mix_out(x2, ylru, yatt, w_out[l].astype(BF16), row(norm_mix_post[l]), row(norm_mlp_pre[l]),
                      w_up[l].astype(BF16), w_down[l].astype(BF16), row(norm_mlp_post[l]))
    return x2.reshape(batch, seq, d)
```

```python
import functools
import math

import jax
import jax.numpy as jnp
from jax import lax
from jax.experimental import pallas as pl
from jax.experimental.pallas import tpu as pltpu

F32 = jnp.float32
BF16 = jnp.bfloat16

NORM_EPS = 1e-6
LRU_C = 8.0
CONV_WIDTH = 4
LRU_BLOCKS = 8
HEAD_DIM = 64
V_DIM = 2 * HEAD_DIM
ROPE_THETA = 10000.0
MASK_VALUE = -1e30
LOG2_E = math.log2(math.e)
ONES_ROWS = 16

SUBLANES = 8
ROW_TILE = 512
LRU_CHUNK = 256
ATT_TQ = 512
ATT_KC = 128
FF_CHUNK = 1024
VMEM_LIMIT = 56 * 1024 * 1024


def _rms(x, g):
    ms = jnp.mean(x * x, axis=-1, keepdims=True)
    return x * lax.rsqrt(ms + NORM_EPS) * g


def _dot(a, b):
    return jnp.dot(a, b, preferred_element_type=F32)


def _mix_in_kernel(x_ref, w_ref, cw_ref, cb_ref, wa_ref, ba_ref, wx_ref, bx_ref, lam_ref,
                   cos_ref, sin_lo_ref, sin_hi_ref,
                   ylru_ref, q_ref, k_ref, v_ref,
                   uext_ref, a_ref, b_ref, gate_ref, hcar_ref, *, lru_w, att_w):
    t = pl.program_id(1)
    tm = x_ref.shape[0]

    @pl.when(t == 0)
    def _():
        uext_ref[0:SUBLANES, :] = jnp.zeros((SUBLANES, lru_w), F32)
        hcar_ref[...] = jnp.zeros_like(hcar_ref)

    x = x_ref[...]
    hn = (x * lax.rsqrt(jnp.mean(x * x, axis=-1, keepdims=True) + NORM_EPS)).astype(BF16)

    c_gate, c_q, c_k, c_v = lru_w, 2 * lru_w, 2 * lru_w + att_w, 2 * lru_w + 2 * att_w
    cos, sin_lo, sin_hi = cos_ref[...], sin_lo_ref[...], sin_hi_ref[...]

    def rope(th):
        up = pltpu.roll(th, V_DIM - HEAD_DIM // 2, 1)
        dn = pltpu.roll(th, HEAD_DIM // 2, 1)
        return th * cos + up * sin_lo + dn * sin_hi

    def proj(col0, c):
        return _dot(hn, w_ref[:, col0 + c * LRU_CHUNK:col0 + (c + 1) * LRU_CHUNK])

    neg_lam = -lam_ref[...]
    softplus = jnp.maximum(neg_lam, 0.0) + jnp.log1p(jnp.exp(-jnp.abs(neg_lam)))

    n_chunks = lru_w // LRU_CHUNK
    for c in range(n_chunks):
        uext_ref[SUBLANES:SUBLANES + tm, c * LRU_CHUNK:(c + 1) * LRU_CHUNK] = proj(0, c)

    for c in range(n_chunks):
        cols = slice(c * LRU_CHUNK, (c + 1) * LRU_CHUNK)
        heads = range(c * LRU_CHUNK // V_DIM, (c + 1) * LRU_CHUNK // V_DIM)
        head_cols = [slice(n * V_DIM, (n + 1) * V_DIM) for n in range(len(heads))]

        xc = cb_ref[:, cols] + cw_ref[CONV_WIDTH - 1:CONV_WIDTH, cols] * uext_ref[SUBLANES:SUBLANES + tm, cols]
        for j in range(1, CONV_WIDTH):
            xc = xc + (cw_ref[CONV_WIDTH - 1 - j:CONV_WIDTH - j, cols]
                       * uext_ref[SUBLANES - j:SUBLANES - j + tm, cols])
        qf = proj(c_q, c)
        xcb = xc.astype(BF16)
        r = jax.nn.sigmoid(_dot(xcb, wa_ref[cols, cols]) + ba_ref[:, cols])
        for h, sl in zip(heads, head_cols):
            q_ref[h] = rope(qf[:, sl]).astype(BF16)
        kf = proj(c_k, c)
        i = jax.nn.sigmoid(_dot(xcb, wx_ref[cols, cols]) + bx_ref[:, cols])
        log_a = (-LRU_C) * r * softplus[:, cols]
        a_ref[:, cols] = jnp.exp(log_a)
        for h, sl in zip(heads, head_cols):
            k_ref[h] = rope(kf[:, sl]).astype(BF16)
        vf = proj(c_v, c)
        th = jnp.tanh(log_a)
        b_ref[:, cols] = (jnp.sqrt(-2.0 * th) * lax.rsqrt(1.0 - th)) * (i * xc)
        for h, sl in zip(heads, head_cols):
            v_ref[h] = vf[:, sl].astype(BF16)
        gate_ref[:, cols] = jax.nn.gelu(proj(c_gate, c))
    uext_ref[0:SUBLANES, :] = uext_ref[tm:tm + SUBLANES, :]

    row = lax.broadcasted_iota(jnp.int32, (SUBLANES, lru_w), 0)
    steps = [(s, row >= s) for s in (1, 2, 4)]

    def group(gidx, hprev):
        r0 = pl.multiple_of(gidx * SUBLANES, SUBLANES)
        a8 = a_ref[pl.ds(r0, SUBLANES), :]
        b8 = b_ref[pl.ds(r0, SUBLANES), :]
        for s, m in steps:
            a_sh = jnp.where(m, pltpu.roll(a8, s, 0), 1.0)
            b_sh = jnp.where(m, pltpu.roll(b8, s, 0), 0.0)
            b8 = a8 * b_sh + b8
            a8 = a8 * a_sh
        h8 = a8 * hprev + b8
        ylru_ref[pl.ds(r0, SUBLANES), :] = (h8 * gate_ref[pl.ds(r0, SUBLANES), :]).astype(BF16)
        return jnp.broadcast_to(h8[SUBLANES - 1:SUBLANES, :], (SUBLANES, lru_w))

    hcar_ref[...] = lax.fori_loop(0, tm // SUBLANES, group, hcar_ref[...], unroll=4)


def _mix_in(x2, w_in, conv_w, conv_b, wa, ba, wx, bx, lam, cos, sin_lo, sin_hi, *, batch, seq):
    n, d = x2.shape
    lru_w = conv_w.shape[1]
    att_w = (w_in.shape[1] - 2 * lru_w) // 3
    tm = ROW_TILE
    nt = seq // tm
    row_blk = lambda w: pl.BlockSpec((tm, w), lambda b, t: (b * nt + t, 0))
    full = lambda a: pl.BlockSpec(a.shape, lambda b, t: (0,) * a.ndim)
    tab = pl.BlockSpec((tm, V_DIM), lambda b, t: (t, 0))
    heads = att_w // V_DIM
    head_major = jax.ShapeDtypeStruct((batch, heads, seq, V_DIM), BF16)
    head_blk = pl.BlockSpec((None, heads, tm, V_DIM), lambda b, t: (b, 0, t, 0))
    return pl.pallas_call(
        functools.partial(_mix_in_kernel, lru_w=lru_w, att_w=att_w),
        grid=(batch, nt),
        in_specs=[row_blk(d), full(w_in), full(conv_w), full(conv_b), full(wa), full(ba),
                  full(wx), full(bx), full(lam), tab, tab, tab],
        out_specs=[row_blk(lru_w), head_blk, head_blk, head_blk],
        out_shape=[jax.ShapeDtypeStruct((n, lru_w), BF16), head_major, head_major, head_major],
        scratch_shapes=[pltpu.VMEM((tm + SUBLANES, lru_w), F32), pltpu.VMEM((tm, lru_w), F32),
                        pltpu.VMEM((tm, lru_w), F32), pltpu.VMEM((tm, lru_w), F32),
                        pltpu.VMEM((SUBLANES, lru_w), F32)],
        compiler_params=pltpu.CompilerParams(dimension_semantics=("arbitrary", "arbitrary"),
                                             vmem_limit_bytes=VMEM_LIMIT),
        name="mix_in",
    )(x2, w_in, conv_w, conv_b, wa, ba, wx, bx, lam, cos, sin_lo, sin_hi)


def _attn_kernel(lq1_ref, lk1_ref, lq2_ref, lk2_ref, gain_ref, q_ref, k_ref, v_ref, o_ref,
                 vt_ref, qm_ref, s_ref, p_ref, acc_ref, *, lam_init):
    seq = k_ref.shape[0]
    tq = tk = ATT_TQ
    nq = seq // tq
    n_chunks = tk // ATT_KC

    vt_ref[V_DIM:, :] = jnp.ones((ONES_ROWS, seq), BF16)
    for c in range(seq // tk):
        cols = slice(c * tk, (c + 1) * tk)
        vt_ref[0:V_DIM, cols] = v_ref[cols, :].astype(F32).T.astype(BF16)

    lam = (jnp.exp(jnp.sum(lq1_ref[...] * lk1_ref[...], axis=-1, keepdims=True))
           - jnp.exp(jnp.sum(lq2_ref[...] * lk2_ref[...], axis=-1, keepdims=True)) + lam_init)

    def load_queries(i):
        qt = q_ref[i * tq:(i + 1) * tq, :].astype(F32).T
        drow = lax.broadcasted_iota(jnp.int32, qt.shape, 0)
        zero = jnp.zeros_like(qt)
        qm_ref[i % 2, 0] = jnp.where(drow < HEAD_DIM, qt, zero).astype(BF16)
        qm_ref[i % 2, 1] = jnp.where(drow >= HEAD_DIM, qt, zero).astype(BF16)

    def stage(score_unit, prob_unit, m_old, mx):
        if prob_unit is not None:
            pi, pj, pm = prob_unit
            m_new = mx if m_old is None else jnp.maximum(m_old, mx)
        mx8 = None
        for c in range(n_chunks):
            rows = slice(c * ATT_KC, (c + 1) * ATT_KC)
            if score_unit is not None:
                si, sj, sm = score_unit
                k0 = sj * tk + c * ATT_KC
                s = _dot(k_ref[k0:k0 + ATT_KC, :], qm_ref[si % 2, sm])
                if sj == si:
                    kpos = lax.broadcasted_iota(jnp.int32, (ATT_KC, tq), 0) + c * ATT_KC
                    qpos = lax.broadcasted_iota(jnp.int32, (ATT_KC, tq), 1)
                    s = jnp.where(kpos <= qpos, s, MASK_VALUE)
                s_ref[sm, rows, :] = s
                part = jnp.max(s.reshape(ATT_KC // SUBLANES, SUBLANES, tq), axis=0)
                mx8 = part if mx8 is None else jnp.maximum(mx8, part)
            if prob_unit is not None:
                p_ref[pm, rows, :] = jnp.exp2(s_ref[pm, rows, :] - m_new).astype(BF16)
        if prob_unit is not None:
            pv = _dot(vt_ref[:, pj * tk:(pj + 1) * tk], p_ref[pm])
            if m_old is None:
                acc_ref[pi % 2, pm] = pv
            else:
                acc_ref[pi % 2, pm] = jnp.exp2(m_old - m_new) * acc_ref[pi % 2, pm] + pv
        new_mx = None if score_unit is None else jnp.max(mx8, axis=0, keepdims=True)
        return new_mx, (m_new if prob_unit is not None else None)

    def finish(i):
        acc1, acc2 = acc_ref.at[i % 2, 0], acc_ref.at[i % 2, 1]
        ot = (acc1[0:V_DIM, :] / acc1[V_DIM:V_DIM + 1, :]
              - lam * (acc2[0:V_DIM, :] / acc2[V_DIM:V_DIM + 1, :]))
        o = ot.T
        o_ref[i * tq:(i + 1) * tq, :] = (_rms(o, gain_ref[...]) * (1.0 - lam_init)).astype(o_ref.dtype)

    units = [(i, j, mp) for i in range(nq) for j in range(i + 1) for mp in range(2)]
    load_queries(0)
    mx, _ = stage(units[0], None, None, None)
    m_run = {}
    for t, unit in enumerate(units):
        i, j, mp = unit
        nxt = units[t + 1] if t + 1 < len(units) else None
        if nxt is not None and nxt[0] != i:
            load_queries(nxt[0])
        mx, m_run[(i, mp)] = stage(nxt, unit, m_run.get((i, mp)), mx)
        if j == i and mp == 1:
            finish(i)


def _attn(lq1, lk1, lq2, lk2, gain, q, k, v, *, lam_init):
    batch, heads, seq, _ = k.shape
    tq = ATT_TQ
    full = lambda a: pl.BlockSpec(a.shape, lambda b, h: (0,) * a.ndim)
    whole_seq = pl.BlockSpec((None, None, seq, V_DIM), lambda b, h: (b, h, 0, 0))
    return pl.pallas_call(
        functools.partial(_attn_kernel, lam_init=lam_init),
        grid=(batch, heads),
        in_specs=[full(lq1), full(lk1), full(lq2), full(lk2), full(gain), whole_seq, whole_seq, whole_seq],
        out_specs=pl.BlockSpec((seq, V_DIM), lambda b, h: (b, h)),
        out_shape=jax.ShapeDtypeStruct((batch * seq, heads * V_DIM), BF16),
        scratch_shapes=[pltpu.VMEM((V_DIM + ONES_ROWS, seq), BF16), pltpu.VMEM((2, 2, V_DIM, tq), BF16),
                        pltpu.VMEM((2, tq, tq), F32), pltpu.VMEM((2, tq, tq), BF16),
                        pltpu.VMEM((2, 2, V_DIM + ONES_ROWS, tq), F32)],
        compiler_params=pltpu.CompilerParams(dimension_semantics=("arbitrary",) * 2,
                                             vmem_limit_bytes=VMEM_LIMIT),
        name="attn",
    )(lq1, lk1, lq2, lk2, gain, q, k, v)


def _mix_out_kernel(x_ref, ylru_ref, yatt_ref, wo_ref, g_post_ref, g_pre2_ref, wu_ref, wd_ref, g_post2_ref,
                    o_ref, acc_ref):
    lru_w = ylru_ref.shape[1]
    tm = x_ref.shape[0]
    d_ff = wu_ref.shape[1]
    halves = [slice(i * (tm // 2), (i + 1) * (tm // 2)) for i in range(2)]

    mixed = [_dot(ylru_ref[rows, :], wo_ref[0:lru_w, :]) + _dot(yatt_ref[rows, :], wo_ref[lru_w:, :])
             for rows in halves]
    h, hm = [None, None], [None, None]

    def norms(i):
        h[i] = x_ref[halves[i], :] + _rms(mixed[i], g_post_ref[...])
        hm[i] = _rms(h[i], g_pre2_ref[...]).astype(BF16)

    def mlp(i, after_first_chunk):
        for c in range(d_ff // FF_CHUNK):
            sl = slice(c * FF_CHUNK, (c + 1) * FF_CHUNK)
            act = jnp.square(jnp.maximum(_dot(hm[i], wu_ref[:, sl]), 0.0)).astype(BF16)
            part = _dot(act, wd_ref[sl, :])
            if c == 0:
                acc_ref[halves[i], :] = part
                after_first_chunk()
            else:
                acc_ref[halves[i], :] += part

    def finish(i):
        o_ref[halves[i], :] = h[i] + _rms(acc_ref[halves[i], :], g_post2_ref[...])

    norms(0)
    mlp(0, after_first_chunk=lambda: norms(1))
    mlp(1, after_first_chunk=lambda: finish(0))
    finish(1)


def _mix_out(x2, ylru, yatt, w_out, g_post, g_pre2, w_up, w_down, g_post2):
    n, d = x2.shape
    tm = ROW_TILE
    row_blk = lambda w: pl.BlockSpec((tm, w), lambda i: (i, 0))
    vec = lambda a: pl.BlockSpec(a.shape, lambda i: (0, 0))
    resident = lambda a: pl.BlockSpec(a.shape, lambda i: (0, 0), pipeline_mode=pl.Buffered(1))
    return pl.pallas_call(
        _mix_out_kernel,
        grid=(n // tm,),
        in_specs=[row_blk(d), row_blk(ylru.shape[1]), row_blk(yatt.shape[1]), resident(w_out), vec(g_post),
                  vec(g_pre2), resident(w_up), resident(w_down), vec(g_post2)],
        out_specs=row_blk(d),
        out_shape=jax.ShapeDtypeStruct((n, d), F32),
        scratch_shapes=[pltpu.VMEM((tm, d), F32)],
        compiler_params=pltpu.CompilerParams(dimension_semantics=("arbitrary",),
                                             vmem_limit_bytes=VMEM_LIMIT),
        name="mix_out",
    )(x2, ylru, yatt, w_out, g_post, g_pre2, w_up, w_down, g_post2)


def _rope_tables(seq):
    half = HEAD_DIM // 2
    freqs = ROPE_THETA ** (-jnp.arange(half, dtype=F32) / half)
    ang = jnp.arange(seq, dtype=F32)[:, None] * freqs[None, :]
    cos, sin = jnp.cos(ang), jnp.sin(ang)
    zero = jnp.zeros_like(sin)
    reps = V_DIM // HEAD_DIM
    cos_t = jnp.tile(jnp.concatenate([cos, cos], axis=1), (1, reps))
    sin_lo = jnp.tile(jnp.concatenate([-sin, zero], axis=1), (1, reps))
    sin_hi = jnp.tile(jnp.concatenate([zero, sin], axis=1), (1, reps))
    return cos_t, sin_lo, sin_hi


def _block_diag(w):
    nb, bs, _ = w.shape
    eye = jnp.eye(nb, dtype=w.dtype)
    return (eye[:, None, :, None] * w[:, :, None, :]).reshape(nb * bs, nb * bs)


def kernel(x, norm_mix_pre, w_in, conv_w, conv_b, w_gate_a, b_gate_a, w_gate_x, b_gate_x, lru_lambda, lambda_q1, lambda_k1, lambda_q2, lambda_k2, att_head_norm, w_out, norm_mix_post, norm_mlp_pre, w_up, w_down, norm_mlp_post):
    batch, seq, d = x.shape
    depth = w_in.shape[0]
    cos, sin_lo, sin_hi = _rope_tables(seq)
    row = lambda a: a.reshape(1, -1)
    x2 = x.reshape(batch * seq, d)
    for l in range(depth):
        lam_init = 0.8 - 0.6 * math.exp(-0.3 * l)
        lru_w = conv_w.shape[2]
        att_w = (w_in.shape[2] - 2 * lru_w) // 3
        col_scale = jnp.ones((w_in.shape[2],), F32).at[2 * lru_w:2 * lru_w + att_w].set(HEAD_DIM ** -0.5 * LOG2_E)
        w_in_l = (w_in[l] * norm_mix_pre[l][:, None] * col_scale[None, :]).astype(BF16)
        ylru, q, k, v = _mix_in(
            x2, w_in_l, conv_w[l], row(conv_b[l]),
            _block_diag(w_gate_a[l]).astype(BF16), row(b_gate_a[l]),
            _block_diag(w_gate_x[l]).astype(BF16), row(b_gate_x[l]), row(lru_lambda[l]),
            cos, sin_lo, sin_hi, batch=batch, seq=seq)
        yatt = _attn(row(lambda_q1[l]), row(lambda_k1[l]), row(lambda_q2[l]), row(lambda_k2[l]),
                     row(att_head_norm[l]), q, k, v, lam_init=lam_init)
        x2 = _mix_out(x2, ylru, yatt, w_out[l].astype(BF16), row(norm_mix_post[l]), row(norm_mlp_pre[l]),
                      w_up[l].astype(BF16), w_down[l].astype(BF16), row(norm_mlp_post[l]))
    return x2.reshape(batch, seq, d)
```
